```python
import math
import jax
import jax.numpy as jnp
from jax import lax
import numpy as np

D_MODEL = 2048
BATCH = 2
SEQ = 4096
DEPTH = 2
DEC_BATCH = 128
DEC_SEQ = 1
PAST_LEN = 8192
PAGE_SIZE = 128

RW_HEADS = 16
RW_HEAD_DIM = 64
RW_WIDTH = RW_HEADS * RW_HEAD_DIM
RW_DECAY_LORA = 96
RW_A_LORA = 96
RW_GATE_LORA = 256
RW_PROJ = 3 * RW_WIDTH + RW_DECAY_LORA + RW_A_LORA + RW_GATE_LORA
RW_DECAY_SCALE = 0.606531
RW_GN_EPS = 64e-5

MLA_HEADS = 8
MLA_NOPE = 128
MLA_ROPE = 64
MLA_QK = MLA_NOPE + MLA_ROPE
MLA_V = 128
MLA_WIDTH = MLA_HEADS * MLA_V
Q_LORA = 512
KV_LORA = 256
MLA_PROJ = Q_LORA + KV_LORA + MLA_ROPE
MLA_SCALE = MLA_QK ** -0.5
ROPE_THETA = 10000.0
Q_BLOCK = 128

EVEN_PROJ = RW_PROJ + MLA_PROJ

HG_HEADS = 16
HG_DK = 128
HG_DV = D_MODEL // HG_HEADS
HG_F = HG_HEADS * HG_DK
HG_PROJ = 2 * HG_F + 2 * D_MODEL
HG_CHUNK = 16

D_FF = 5632
CONV_W = 3
NORM_EPS = 1e-6

kernel_name = 'hybrid_rwkv7_mla_hgrn2_convffn_step'


def rmsnorm(x, g):
    xf = x.astype(jnp.float32)
    y = xf * lax.rsqrt(jnp.mean(xf * xf, axis=-1, keepdims=True) + NORM_EPS)
    return (y * g.astype(jnp.float32)).astype(x.dtype)


def apply_rope(t, pos):
    half = MLA_ROPE // 2
    inv = ROPE_THETA ** (-jnp.arange(half, dtype=jnp.float32) / half)
    ang = pos.astype(jnp.float32)[:, None] * inv
    cos = jnp.cos(ang)[:, None, :]
    sin = jnp.sin(ang)[:, None, :]
    tf = t.astype(jnp.float32)
    nope = tf[..., :MLA_NOPE]
    r1 = tf[..., MLA_NOPE:MLA_NOPE + half]
    r2 = tf[..., MLA_NOPE + half:]
    return jnp.concatenate([nope, r1 * cos - r2 * sin, r1 * sin + r2 * cos], axis=-1).astype(t.dtype)


def mla_keys(c, kr, pos, w_uk, g_k):
    B, T, _ = c.shape
    k_nope = (c @ w_uk).reshape(B, T, MLA_HEADS, MLA_NOPE)
    k_rot = jnp.broadcast_to(kr[:, :, None, :], (B, T, MLA_HEADS, MLA_ROPE))
    return apply_rope(rmsnorm(jnp.concatenate([k_nope, k_rot], axis=-1), g_k), pos)


def blocked_causal_attention(q, k, c):
    B, L, H, _ = q.shape
    qb_len = math.gcd(L, Q_BLOCK)
    nb = L // qb_len
    qb = q.reshape(B, nb, qb_len, H, MLA_QK).swapaxes(0, 1)
    kpos = jnp.arange(L, dtype=jnp.int32)

    def block(args):
        qi, i = args
        s = jnp.einsum('bqhd,bkhd->bhqk', qi, k, preferred_element_type=jnp.float32) * MLA_SCALE
        qpos = i * qb_len + jnp.arange(qb_len, dtype=jnp.int32)
        s = jnp.where(kpos[None, :] <= qpos[:, None], s, -jnp.inf)
        pr = jax.nn.softmax(s, axis=-1)
        return jnp.einsum('bhqk,bkc->bqhc', pr, c.astype(jnp.float32))

    o = lax.map(block, (qb, jnp.arange(nb, dtype=jnp.int32)))
    return o.swapaxes(0, 1).reshape(B, L, H, KV_LORA).astype(q.dtype)


def paged_attention(q, k_new, c_new, cache_lat, cache_kr, page_table, w_uk, g_k):
    f32 = jnp.float32
    n_pages = page_table.shape[1]
    Sq = q.shape[1]

    def page_block(args):
        phys, pidx = args
        c = cache_lat[phys]
        kr = cache_kr[phys]
        pos = pidx * PAGE_SIZE + jnp.arange(PAGE_SIZE, dtype=jnp.int32)
        k = mla_keys(c, kr, pos, w_uk, g_k)
        s = jnp.einsum('bqhd,bkhd->bhqk', q, k, preferred_element_type=f32) * MLA_SCALE
        m = jnp.max(s, axis=-1)
        e = jnp.exp(s - m[..., None])
        return m, jnp.sum(e, axis=-1), jnp.einsum('bhqk,bkc->bhqc', e, c.astype(f32))

    m_p, l_p, a_p = lax.map(page_block, (page_table.T, jnp.arange(n_pages, dtype=jnp.int32)))
    s = jnp.einsum('bqhd,bkhd->bhqk', q, k_new, preferred_element_type=f32) * MLA_SCALE
    causal = jnp.tril(jnp.ones((Sq, Sq), dtype=bool))
    s = jnp.where(causal, s, -jnp.inf)
    m_s = jnp.max(s, axis=-1)
    e_s = jnp.exp(s - m_s[..., None])
    m = jnp.maximum(jnp.max(m_p, axis=0), m_s)
    w_p = jnp.exp(m_p - m)
    w_s = jnp.exp(m_s - m)
    l = jnp.sum(l_p * w_p, axis=0) + jnp.sum(e_s, axis=-1) * w_s
    acc = (jnp.sum(a_p * w_p[..., None], axis=0)
           + jnp.einsum('bhqk,bkc->bhqc', e_s, c_new.astype(f32)) * w_s[..., None])
    return (acc / l[..., None]).transpose(0, 2, 1, 3).astype(q.dtype)


def mla_mixer(p, pos, past, W):
    B, L, _ = p.shape
    q = rmsnorm(p[..., :Q_LORA], W['mla_q_norm']) @ W['mla_w_qb']
    q = apply_rope(rmsnorm(q.reshape(B, L, MLA_HEADS, MLA_QK), W['mla_qk_norm_q']), pos)
    c = rmsnorm(p[..., Q_LORA:Q_LORA + KV_LORA], W['mla_kv_norm'])
    kr = p[..., Q_LORA + KV_LORA:]
    k = mla_keys(c, kr, pos, W['mla_w_uk'], W['mla_qk_norm_k'])
    if past is None:
        o_lat = blocked_causal_attention(q, k, c)
    else:
        cache_lat, cache_kr, page_table = past
        o_lat = paged_attention(q, k, c, cache_lat, cache_kr, page_table, W['mla_w_uk'], W['mla_qk_norm_k'])
    o = jnp.einsum('blhc,hcv->blhv', o_lat, W['mla_w_uv'])
    return o.reshape(B, L, MLA_WIDTH), c, kr


def rwkv7_scan(r, w, k, v, kk, a, S0):
    def step(S, xs):
        r_t, w_t, k_t, v_t, kk_t, a_t = xs
        sa = jnp.einsum('bhvk,bhk->bhv', S, -kk_t)
        S = (S * w_t[:, :, None, :] + sa[..., None] * (kk_t * a_t)[:, :, None, :]
             + v_t[..., None] * k_t[:, :, None, :])
        return S, jnp.einsum('bhvk,bhk->bhv', S, r_t)

    xs = tuple(t.swapaxes(0, 1) for t in (r, w, k, v, kk, a))
    S, o = lax.scan(step, S0, xs)
    return o.swapaxes(0, 1), S


def rwkv7_mixer(p, shift_prev, S0, W):
    f32 = jnp.float32
    B, L, _ = p.shape
    prev = jnp.concatenate([shift_prev[:, None, :].astype(p.dtype), p[:, :-1]], axis=1)
    ps = (p + W['rw_mu'] * (prev - p)).astype(f32)
    o1 = 3 * RW_WIDTH
    o2 = o1 + RW_DECAY_LORA
    o3 = o2 + RW_A_LORA
    r = ps[..., :RW_WIDTH]
    k = ps[..., RW_WIDTH:2 * RW_WIDTH]
    v = ps[..., 2 * RW_WIDTH:o1]
    w_logit = W['rw_w0'] + jnp.tanh(ps[..., o1:o2]) @ W['rw_w_dec']
    decay = jnp.exp(-RW_DECAY_SCALE * jax.nn.sigmoid(w_logit.astype(f32)))
    a = jax.nn.sigmoid(W['rw_a0'] + ps[..., o2:o3] @ W['rw_w_a']).astype(f32)
    g = (jax.nn.sigmoid(ps[..., o3:]) @ W['rw_w_g']).astype(f32)
    heads = lambda t: t.reshape(B, L, RW_HEADS, RW_HEAD_DIM)
    kk = heads(k * W['rw_k_k'])
    kk = kk / jnp.maximum(jnp.sqrt(jnp.sum(kk * kk, axis=-1, keepdims=True)), 1e-12)
    k = k * (1.0 + (a - 1.0) * W['rw_k_a'])
    r, k, v, decay, a = heads(r), heads(k), heads(v), heads(decay), heads(a)
    o, S = rwkv7_scan(r, decay, k, v, kk, a, S0.astype(f32))
    mu = jnp.mean(o, axis=-1, keepdims=True)
    var = jnp.mean(jnp.square(o - mu), axis=-1, keepdims=True)
    o = ((o - mu) * lax.rsqrt(var + RW_GN_EPS)).reshape(B, L, RW_WIDTH) * W['rw_gn_g'] + W['rw_gn_b']
    bonus = jnp.sum(r * k * W['rw_r_k'], axis=-1, keepdims=True) * v
    o = (o + bonus.reshape(B, L, RW_WIDTH)) * g
    return o.astype(p.dtype), S.astype(S0.dtype), p[:, -1]


def hgrn2_chunked(q, k, v, logf, S0):
    B, L, H, _ = q.shape
    C = math.gcd(L, HG_CHUNK)
    nc = L // C
    resh = lambda t: t.reshape(B, nc, C, H, t.shape[-1]).transpose(1, 0, 3, 2, 4)
    qc, kc, vc, gc = resh(q), resh(k), resh(v), resh(logf)
    b = jnp.cumsum(gc, axis=3)
    q_dec = qc * jnp.exp(b)
    k_inv = kc * jnp.exp(-b)
    k_end = kc * jnp.exp(b[..., -1:, :] - b)
    mask = jnp.tril(jnp.ones((C, C), dtype=q.dtype))
    A = jnp.einsum('nbhtd,nbhsd->nbhts', q_dec, k_inv) * mask
    o_intra = jnp.einsum('nbhts,nbhsv->nbhtv', A, vc)

    def step(S, xs):
        qd, ke, vv, bend = xs
        o_inter = jnp.einsum('bhtd,bhdv->bhtv', qd, S)
        S = S * jnp.exp(bend)[..., None] + jnp.einsum('bhsd,bhsv->bhdv', ke, vv)
        return S, o_inter

    S, o_inter = lax.scan(step, S0, (q_dec, k_end, vc, b[:, :, :, -1, :]))
    o = o_intra + o_inter
    return o.transpose(1, 0, 3, 2, 4).reshape(B, L, H, v.shape[-1]), S


def hgrn2_mixer(p, S0, layer, hg_lower_bounds, hg_norm_g):
    f32 = jnp.float32
    B, L, _ = p.shape
    pf = p.astype(f32)
    q = jax.nn.silu(pf[..., :HG_F])
    f_logit = pf[..., HG_F:2 * HG_F]
    i = pf[..., 2 * HG_F:2 * HG_F + D_MODEL]
    g = pf[..., 2 * HG_F + D_MODEL:]
    lb_all = jnp.cumsum(jax.nn.softmax(hg_lower_bounds.astype(f32), axis=0), axis=0)
    lb = lb_all[layer] - lb_all[0]
    f = lb + (1.0 - lb) * jax.nn.sigmoid(f_logit)
    hk = lambda t: t.reshape(B, L, HG_HEADS, HG_DK)
    o, S = hgrn2_chunked(hk(q), hk(1.0 - f), i.reshape(B, L, HG_HEADS, HG_DV), hk(jnp.log(f)), S0.astype(f32))
    o = rmsnorm(o, hg_norm_g).reshape(B, L, D_MODEL) * jax.nn.sigmoid(g)
    return o.astype(p.dtype), S.astype(S0.dtype)


def conv_ffn(x, conv_prev, norm_g, w_in, conv_w, conv_b, w_out):
    L = x.shape[1]
    u = rmsnorm(x, norm_g) @ w_in
    u_g, u_u = u[..., :D_FF], u[..., D_FF:]
    full = jnp.concatenate([conv_prev.astype(u.dtype), u_g], axis=1)
    conv = conv_b + sum(conv_w[j] * full[:, j:j + L] for j in range(CONV_W))
    y = jax.nn.silu(conv) * u_u
    return x + y @ w_out, full[:, L:]


def trunk(x, pos, rw_S0, rw_shift0, hg_S0, conv0, past, W):
    conv_rows = []
    for layer in range(DEPTH):
        h = rmsnorm(x, W['norm_mix_g'][layer])
        if layer % 2 == 0:
            p = h @ W['w_in_even']
            o_rw, rw_S, rw_shift = rwkv7_mixer(p[..., :RW_PROJ], rw_shift0, rw_S0, W)
            o_mla, c_lat, k_rope = mla_mixer(p[..., RW_PROJ:], pos, past, W)
            o = jnp.concatenate([o_rw, o_mla], axis=-1) @ W['w_out_even']
        else:
            o, hg_S = hgrn2_mixer(h @ W['w_in_odd'], hg_S0, layer, W['hg_lower_bounds'], W['hg_norm_g'])
            o = o @ W['w_out_odd']
        x = x + o
        x, conv_new = conv_ffn(x, conv0[layer], W['norm_ffn_g'][layer], W['ffn_w_in'][layer],
                               W['ffn_conv_w'][layer], W['ffn_conv_b'][layer], W['ffn_w_out'][layer])
        conv_rows.append(conv_new)
    return x, c_lat, k_rope, rw_S, rw_shift, hg_S, jnp.stack(conv_rows)


def setup_inputs(seed: int = 0) -> dict:
    key = jax.random.key(seed)
    ks = iter(jax.random.split(key, 64))
    f32 = jnp.float32
    nrm = lambda shape, s=1.0: s * jax.random.normal(next(ks), shape, f32)
    gain = lambda shape: 1.0 + 0.1 * jax.random.normal(next(ks), shape, f32)
    n_pages = PAST_LEN // PAGE_SIZE
    n_pool = (5 * DEC_BATCH * n_pages) // 4
    page_table = jax.random.permutation(next(ks), n_pool)[:DEC_BATCH * n_pages]
    page_table = page_table.reshape(DEC_BATCH, n_pages).astype(jnp.int32)
    return {
        'x_prompt': nrm((BATCH, SEQ, D_MODEL)),
        'x_sample': nrm((DEC_BATCH, DEC_SEQ, D_MODEL)),
        'cache_mla_latent': nrm((n_pool, PAGE_SIZE, KV_LORA)),
        'cache_mla_krope': nrm((n_pool, PAGE_SIZE, MLA_ROPE)),
        'state_rwkv': nrm((DEC_BATCH, RW_HEADS, RW_HEAD_DIM, RW_HEAD_DIM), 0.3),
        'state_rwkv_shift': nrm((DEC_BATCH, RW_PROJ)),
        'state_hgrn': nrm((DEC_BATCH, HG_HEADS, HG_DK, HG_DV), 0.3),
        'state_ffn_conv': nrm((DEPTH, DEC_BATCH, CONV_W - 1, D_FF)),
        'page_table': page_table,
        'norm_mix_g': gain((DEPTH, D_MODEL)),
        'norm_ffn_g': gain((DEPTH, D_MODEL)),
        'w_in_even': nrm((D_MODEL, EVEN_PROJ), D_MODEL ** -0.5),
        'rw_mu': jax.random.uniform(next(ks), (RW_PROJ,), f32),
        'rw_w0': nrm((RW_WIDTH,), 0.5),
        'rw_w_dec': nrm((RW_DECAY_LORA, RW_WIDTH), RW_DECAY_LORA ** -0.5),
        'rw_a0': nrm((RW_WIDTH,), 0.1),
        'rw_w_a': nrm((RW_A_LORA, RW_WIDTH), RW_A_LORA ** -0.5),
        'rw_w_g': nrm((RW_GATE_LORA, RW_WIDTH), RW_GATE_LORA ** -0.5),
        'rw_k_k': gain((RW_WIDTH,)),
        'rw_k_a': gain((RW_WIDTH,)),
        'rw_r_k': nrm((RW_HEADS, RW_HEAD_DIM), 0.1),
        'rw_gn_g': gain((RW_WIDTH,)),
        'rw_gn_b': nrm((RW_WIDTH,), 0.01),
        'mla_q_norm': gain((Q_LORA,)),
        'mla_w_qb': nrm((Q_LORA, MLA_HEADS * MLA_QK), Q_LORA ** -0.5),
        'mla_kv_norm': gain((KV_LORA,)),
        'mla_w_uk': nrm((KV_LORA, MLA_HEADS * MLA_NOPE), KV_LORA ** -0.5),
        'mla_w_uv': nrm((MLA_HEADS, KV_LORA, MLA_V), KV_LORA ** -0.5),
        'mla_qk_norm_q': gain((MLA_QK,)),
        'mla_qk_norm_k': gain((MLA_QK,)),
        'w_out_even': nrm((RW_WIDTH + MLA_WIDTH, D_MODEL), (RW_WIDTH + MLA_WIDTH) ** -0.5),
        'w_in_odd': nrm((D_MODEL, HG_PROJ), D_MODEL ** -0.5),
        'hg_lower_bounds': nrm((DEPTH, HG_F), 0.1),
        'hg_norm_g': gain((HG_DV,)),
        'w_out_odd': nrm((D_MODEL, D_MODEL), D_MODEL ** -0.5),
        'ffn_w_in': nrm((DEPTH, D_MODEL, 2 * D_FF), D_MODEL ** -0.5),
        'ffn_conv_w': nrm((DEPTH, CONV_W, D_FF), CONV_W ** -0.5),
        'ffn_conv_b': nrm((DEPTH, D_FF), 0.01),
        'ffn_w_out': nrm((DEPTH, D_FF, D_MODEL), D_FF ** -0.5),
    }


def reference(x_prompt, x_sample, cache_mla_latent, cache_mla_krope, state_rwkv, state_rwkv_shift,
              state_hgrn, state_ffn_conv, page_table,
              norm_mix_g, norm_ffn_g, w_in_even, rw_mu, rw_w0, rw_w_dec, rw_a0, rw_w_a, rw_w_g,
              rw_k_k, rw_k_a, rw_r_k, rw_gn_g, rw_gn_b, mla_q_norm, mla_w_qb, mla_kv_norm, mla_w_uk,
              mla_w_uv, mla_qk_norm_q, mla_qk_norm_k, w_out_even, w_in_odd, hg_lower_bounds,
              hg_norm_g, w_out_odd, ffn_w_in, ffn_conv_w, ffn_conv_b, ffn_w_out):
    W = dict(norm_mix_g=norm_mix_g, norm_ffn_g=norm_ffn_g, w_in_even=w_in_even, rw_mu=rw_mu,
             rw_w0=rw_w0, rw_w_dec=rw_w_dec, rw_a0=rw_a0, rw_w_a=rw_w_a, rw_w_g=rw_w_g,
             rw_k_k=rw_k_k, rw_k_a=rw_k_a, rw_r_k=rw_r_k, rw_gn_g=rw_gn_g, rw_gn_b=rw_gn_b,
             mla_q_norm=mla_q_norm, mla_w_qb=mla_w_qb, mla_kv_norm=mla_kv_norm, mla_w_uk=mla_w_uk,
             mla_w_uv=mla_w_uv, mla_qk_norm_q=mla_qk_norm_q, mla_qk_norm_k=mla_qk_norm_k,
             w_out_even=w_out_even, w_in_odd=w_in_odd, hg_lower_bounds=hg_lower_bounds,
             hg_norm_g=hg_norm_g, w_out_odd=w_out_odd, ffn_w_in=ffn_w_in, ffn_conv_w=ffn_conv_w,
             ffn_conv_b=ffn_conv_b, ffn_w_out=ffn_w_out)
    B, L, _ = x_prompt.shape
    dt = x_prompt.dtype
    pos_p = jnp.arange(L, dtype=jnp.int32)
    y_prompt, lat_p, kr_p, rw_p, sh_p, hg_p, cv_p = trunk(
        x_prompt, pos_p,
        jnp.zeros((B, RW_HEADS, RW_HEAD_DIM, RW_HEAD_DIM), dt),
        jnp.zeros((B, RW_PROJ), dt),
        jnp.zeros((B, HG_HEADS, HG_DK, HG_DV), dt),
        jnp.zeros((DEPTH, B, CONV_W - 1, D_FF), dt),
        None, W)
    pos_s = PAST_LEN + jnp.arange(x_sample.shape[1], dtype=jnp.int32)
    y_sample, lat_s, kr_s, rw_s, sh_s, hg_s, cv_s = trunk(
        x_sample, pos_s, state_rwkv, state_rwkv_shift, state_hgrn, state_ffn_conv,
        (cache_mla_latent, cache_mla_krope, page_table), W)
    return (y_prompt, y_sample, lat_p, lat_s, kr_p, kr_s, rw_p, rw_s, sh_p, sh_s, hg_p, hg_s, cv_p, cv_s)
```

```python
import functools
import math

import jax
import jax.numpy as jnp
from jax import lax
from jax.experimental import pallas as pl
from jax.experimental.pallas import tpu as pltpu

F32 = jnp.float32
BF16 = jnp.bfloat16
HI = lax.Precision.HIGHEST

D_MODEL = 2048
RW_HEADS = 16
RW_HEAD_DIM = 64
RW_WIDTH = RW_HEADS * RW_HEAD_DIM
RW_DECAY_LORA = 96
RW_A_LORA = 96
RW_GATE_LORA = 256
RW_PROJ = 3 * RW_WIDTH + RW_DECAY_LORA + RW_A_LORA + RW_GATE_LORA
RW_DECAY_SCALE = 0.606531
RW_GN_EPS = 64e-5
MLA_HEADS = 8
MLA_NOPE = 128
MLA_ROPE = 64
MLA_QK = MLA_NOPE + MLA_ROPE
MLA_V = 128
MLA_WIDTH = MLA_HEADS * MLA_V
Q_LORA = 512
KV_LORA = 256
MLA_PROJ = Q_LORA + KV_LORA + MLA_ROPE
MLA_SCALE = MLA_QK ** -0.5
ROPE_THETA = 10000.0
HG_HEADS = 16
HG_DK = 128
HG_DV = D_MODEL // HG_HEADS
HG_F = HG_HEADS * HG_DK
D_FF = 5632
CONV_W = 3
NORM_EPS = 1e-6
PAGE_SIZE = 128

LANE = 128
RW_LORA_PAD = 128
RW_PAD = 3 * RW_WIDTH + 2 * RW_LORA_PAD + RW_GATE_LORA
MLA_PAD = 896
EVEN_PAD = RW_PAD + MLA_PAD
MLA_HEAD_PAD = 256
VMEM_LIMIT = 56 * 1024 * 1024


def _cparams(sem):
    return pltpu.CompilerParams(dimension_semantics=sem, vmem_limit_bytes=VMEM_LIMIT)


def _dot(a, b):
    return jnp.dot(a, b, preferred_element_type=F32)


def _dot_nt(a, b):
    return lax.dot_general(a, b, (((1,), (1,)), ((), ())), preferred_element_type=F32)


def _dot_tn(a, b):
    return lax.dot_general(a, b, (((0,), (0,)), ((), ())), preferred_element_type=F32)


def _bdot(a, b):
    return lax.dot_general(a, b, (((2,), (1,)), ((0,), (0,))), preferred_element_type=F32)


def _bdot_nt(a, b):
    return lax.dot_general(a, b, (((2,), (2,)), ((0,), (0,))), preferred_element_type=F32)


def _bdot_tn(a, b):
    return lax.dot_general(a, b, (((1,), (1,)), ((0,), (0,))), preferred_element_type=F32)


def _rms_rows(x, g):
    return x * lax.rsqrt(jnp.mean(x * x, axis=-1, keepdims=True) + NORM_EPS) * g


def _norm_matmul_kernel(x_ref, g_ref, w_ref, o_ref, xn_ref):
    @pl.when(pl.program_id(1) == 0)
    def _():
        xn_ref[...] = _rms_rows(x_ref[...], g_ref[...]).astype(BF16)

    o_ref[...] = _dot(xn_ref[...], w_ref[...])


def norm_matmul(x, g, w, *, tm, tn):
    M, K = x.shape
    N = w.shape[1]
    assert M % tm == 0 and N % tn == 0
    return pl.pallas_call(
        _norm_matmul_kernel,
        grid=(M // tm, N // tn),
        in_specs=[
            pl.BlockSpec((tm, K), lambda i, j: (i, 0)),
            pl.BlockSpec((1, K), lambda i, j: (0, 0)),
            pl.BlockSpec((K, tn), lambda i, j: (0, j)),
        ],
        out_specs=pl.BlockSpec((tm, tn), lambda i, j: (i, j)),
        out_shape=jax.ShapeDtypeStruct((M, N), F32),
        scratch_shapes=[pltpu.VMEM((tm, K), BF16)],
        compiler_params=_cparams(("parallel", "arbitrary")),
        name="norm_matmul",
    )(x, g.reshape(1, K), w)


def _matmul_res_kernel(*refs, n_pairs):
    res_ref = refs[2 * n_pairs]
    o_ref = refs[2 * n_pairs + 1]
    acc = res_ref[...]
    for p in range(n_pairs):
        acc = acc + _dot(refs[2 * p][...], refs[2 * p + 1][...])
    o_ref[...] = acc


def matmul_res(pairs, res, *, tm, tn):
    M, N = res.shape
    assert M % tm == 0 and N % tn == 0
    in_specs, args = [], []
    for a, w in pairs:
        K = a.shape[1]
        in_specs += [pl.BlockSpec((tm, K), lambda i, j: (i, 0)), pl.BlockSpec((K, tn), lambda i, j: (0, j))]
        args += [a, w]
    in_specs.append(pl.BlockSpec((tm, tn), lambda i, j: (i, j)))
    return pl.pallas_call(
        functools.partial(_matmul_res_kernel, n_pairs=len(pairs)),
        grid=(M // tm, N // tn),
        in_specs=in_specs,
        out_specs=pl.BlockSpec((tm, tn), lambda i, j: (i, j)),
        out_shape=jax.ShapeDtypeStruct((M, N), F32),
        compiler_params=_cparams(("parallel", "arbitrary")),
        name="matmul_res",
    )(*args, res)


def _ffn_kernel(x_ref, g_ref, wg_ref, wu_ref, cw_ref, cb_ref, wo_ref, prev_ref, y_ref, tail_ref,
                xn_ref, acc_ref, carry_ref, *, seq_tiles, step_mode):
    i = pl.program_id(0)
    f = pl.program_id(1)
    tm = x_ref.shape[0]

    @pl.when(f == 0)
    def _():
        xn_ref[...] = _rms_rows(x_ref[...], g_ref[...]).astype(BF16)
        acc_ref[...] = jnp.zeros_like(acc_ref)

    xn = xn_ref[...]
    ug = _dot(xn, wg_ref[...])
    uu = _dot(xn, wu_ref[...])
    if step_mode:
        p0 = prev_ref[:, 0, :]
        p1 = prev_ref[:, 1, :]
        tail_ref[...] = ug
    else:
        first = (i % seq_tiles) == 0
        c = jnp.where(first, prev_ref[0], carry_ref[f])
        row = lax.broadcasted_iota(jnp.int32, ug.shape, 0)
        p1 = jnp.where(row == 0, c[1:2, :], pltpu.roll(ug, 1, axis=0))
        p0 = jnp.where(row == 0, c[0:1, :], jnp.where(row == 1, c[1:2, :], pltpu.roll(ug, 2, axis=0)))
        tail = ug[tm - 2:tm, :]
        carry_ref[f] = tail
        tail_ref[0] = tail
    cw = cw_ref[...]
    conv = cb_ref[...] + cw[0:1, :] * p0 + cw[1:2, :] * p1 + cw[2:3, :] * ug
    y = (conv * jax.nn.sigmoid(conv)) * uu
    acc_ref[...] += _dot(y.astype(BF16), wo_ref[...])

    @pl.when(f == pl.num_programs(1) - 1)
    def _():
        y_ref[...] = x_ref[...] + acc_ref[...]


def conv_ffn(x, conv_prev, g, w_in, conv_w, conv_b, w_out, *, seq_len, tm, tf):
    M, D = x.shape
    step_mode = seq_len == 1
    nf = D_FF // tf
    assert M % tm == 0 and D_FF % tf == 0 and (step_mode or seq_len % tm == 0)
    seq_tiles = 1 if step_mode else seq_len // tm
    if step_mode:
        prev_spec = pl.BlockSpec((tm, 2, tf), lambda i, f: (i, 0, f))
        tail_spec = pl.BlockSpec((tm, tf), lambda i, f: (i, f))
        tail_shape = jax.ShapeDtypeStruct((M, D_FF), F32)
    else:
        prev_spec = pl.BlockSpec((1, 2, tf), lambda i, f: (i // seq_tiles, 0, f))
        tail_spec = pl.BlockSpec((1, 2, tf), lambda i, f: (i, 0, f))
        tail_shape = jax.ShapeDtypeStruct((M // tm, 2, D_FF), F32)
    y, tail = pl.pallas_call(
        functools.partial(_ffn_kernel, seq_tiles=seq_tiles, step_mode=step_mode),
        grid=(M // tm, nf),
        in_specs=[
            pl.BlockSpec((tm, D), lambda i, f: (i, 0)),
            pl.BlockSpec((1, D), lambda i, f: (0, 0)),
            pl.BlockSpec((D, tf), lambda i, f: (0, f)),
            pl.BlockSpec((D, tf), lambda i, f: (0, f + nf)),
            pl.BlockSpec((CONV_W, tf), lambda i, f: (0, f)),
            pl.BlockSpec((1, tf), lambda i, f: (0, f)),
            pl.BlockSpec((tf, D), lambda i, f: (f, 0)),
            prev_spec,
        ],
        out_specs=[pl.BlockSpec((tm, D), lambda i, f: (i, 0)), tail_spec],
        out_shape=[jax.ShapeDtypeStruct((M, D), F32), tail_shape],
        scratch_shapes=[pltpu.VMEM((tm, D), BF16), pltpu.VMEM((tm, D), F32), pltpu.VMEM((nf, 2, tf), F32)],
        compiler_params=_cparams(("arbitrary", "arbitrary")),
        name="conv_ffn",
    )(x, g.reshape(1, D), w_in, w_in, conv_w, conv_b.reshape(1, D_FF), w_out, conv_prev)
    if not step_mode:
        tail = tail[seq_tiles - 1::seq_tiles]
    return y, tail


def _rw_prep_kernel(p_ref, sh_ref, mu_ref, w0_ref, wdec_ref, a0_ref, wa_ref, wg_ref, kk_ref, ka_ref, rk_ref,
                    seg_ref, segt_ref, r_out, lw_out, k_out, v_out, kk_out, b_out, g_out, bg_out, carry_ref,
                    *, seq_tiles, step_mode):
    i = pl.program_id(0)
    p = p_ref[...]
    if step_mode:
        prev = sh_ref[...]
    else:
        first = (i % seq_tiles) == 0
        c = jnp.where(first, sh_ref[0], carry_ref[...])
        row = lax.broadcasted_iota(jnp.int32, p.shape, 0)
        prev = jnp.where(row == 0, c, pltpu.roll(p, 1, axis=0))
        carry_ref[...] = p[p.shape[0] - 1:, :]
    ps = p + mu_ref[...] * (prev - p)
    W = RW_WIDTH
    r = ps[:, :W]
    k = ps[:, W:2 * W]
    v = ps[:, 2 * W:3 * W]
    o1 = 3 * W
    o2 = o1 + RW_LORA_PAD
    o3 = o2 + RW_LORA_PAD
    w_logit = w0_ref[...] + _dot(jnp.tanh(ps[:, o1:o2]).astype(BF16), wdec_ref[...])
    lw = -RW_DECAY_SCALE * jax.nn.sigmoid(w_logit)
    a = jax.nn.sigmoid(a0_ref[...] + _dot(ps[:, o2:o3].astype(BF16), wa_ref[...]))
    g = _dot(jax.nn.sigmoid(ps[:, o3:]).astype(BF16), wg_ref[...])
    seg = seg_ref[...]
    segt = segt_ref[...]
    head_sum = lambda t: jnp.dot(jnp.dot(t, seg, precision=HI, preferred_element_type=F32), segt,
                                 precision=HI, preferred_element_type=F32)
    kk = k * kk_ref[...]
    kk = kk / jnp.maximum(jnp.sqrt(head_sum(kk * kk)), 1e-12)
    k = k * (1.0 + (a - 1.0) * ka_ref[...])
    bonus = head_sum(r * k * rk_ref[...]) * v
    r_out[...] = r
    lw_out[...] = lw
    k_out[...] = k
    v_out[...] = v
    kk_out[...] = kk
    b_out[...] = kk * a
    g_out[...] = g
    bg_out[...] = bonus * g


def rwkv_prep(P, shift_prev, wts, *, seq_len, tm):
    M = P.shape[0]
    step_mode = seq_len == 1
    seq_tiles = 1 if step_mode else seq_len // tm
    assert M % tm == 0 and (step_mode or seq_len % tm == 0)
    W = RW_WIDTH
    if step_mode:
        sh = shift_prev
        sh_spec = pl.BlockSpec((tm, RW_PAD), lambda i: (i, 0))
    else:
        sh = shift_prev.reshape(-1, 1, RW_PAD)
        sh_spec = pl.BlockSpec((1, 1, RW_PAD), lambda i: (i // seq_tiles, 0, 0))
    full = lambda shape: pl.BlockSpec(shape, lambda i: (0,) * len(shape))
    out = jax.ShapeDtypeStruct((M, W), F32)
    return pl.pallas_call(
        functools.partial(_rw_prep_kernel, seq_tiles=seq_tiles, step_mode=step_mode),
        grid=(M // tm,),
        in_specs=[
            pl.BlockSpec((tm, RW_PAD), lambda i: (i, 0)), sh_spec,
            full((1, RW_PAD)), full((1, W)), full((RW_LORA_PAD, W)), full((1, W)), full((RW_LORA_PAD, W)),
            full((RW_GATE_LORA, W)), full((1, W)), full((1, W)), full((1, W)), full((W, LANE)), full((LANE, W)),
        ],
        out_specs=[pl.BlockSpec((tm, W), lambda i: (i, 0))] * 8,
        out_shape=[out] * 8,
        scratch_shapes=[pltpu.VMEM((1, RW_PAD), F32)],
        compiler_params=_cparams(("arbitrary",)),
        name="rwkv_prep",
    )(P, sh, wts["mu"], wts["w0"], wts["w_dec"], wts["a0"], wts["w_a"], wts["w_g"], wts["k_k"], wts["k_a"],
      wts["r_k"], wts["seg"], wts["segt"])


def _unit_lower_inverse(n):
    C = n.shape[-1]
    row = lax.broadcasted_iota(jnp.int32, n.shape, 1)
    col = lax.broadcasted_iota(jnp.int32, n.shape, 2)
    x = jnp.where(row == col, 1.0, 0.0) - n
    pw = n
    for _ in range(int(math.log2(C)) - 1):
        pw = _bdot(pw.astype(BF16), pw.astype(BF16))
        x = x + _bdot(x.astype(BF16), pw.astype(BF16))
    return x


def _rw_scan_kernel(r_ref, lw_ref, k_ref, v_ref, kk_ref, b_ref, g_ref, bg_ref, s0_ref, gng_ref, gnb_ref,
                    o_ref, s_ref, *, valid_len):
    C = r_ref.shape[1]
    H, N = RW_HEADS, RW_HEAD_DIM

    @pl.when(pl.program_id(1) == 0)
    def _():
        s_ref[...] = s0_ref[...]

    r, lw, k, v, kk, b = r_ref[0], lw_ref[0], k_ref[0], v_ref[0], kk_ref[0], b_ref[0]
    if valid_len < C:
        valid = lax.broadcasted_iota(jnp.int32, r.shape, 0) < valid_len
        lw = jnp.where(valid, lw, 0.0)
        k = jnp.where(valid, k, 0.0)
        kk = jnp.where(valid, kk, 0.0)
        b = jnp.where(valid, b, 0.0)
    trow = lax.broadcasted_iota(jnp.int32, (C, C), 0)
    tcol = lax.broadcasted_iota(jnp.int32, (C, C), 1)
    tri = jnp.where(tcol <= trow, 1.0, 0.0)
    lp = jnp.dot(tri, lw, precision=HI, preferred_element_type=F32)
    lpe = lp[C - 1:C, :]
    p_in = jnp.exp(lp)
    p_ex = jnp.exp(lp - lw)
    p_inv = jnp.exp(-lp)
    p_end = jnp.exp(lpe - lp)
    heads = lambda t: jnp.stack([t[:, h * N:(h + 1) * N] for h in range(H)], axis=0)
    X = heads(jnp.concatenate([kk * p_ex, r * p_in], axis=0).astype(BF16))
    Y = heads(jnp.concatenate([k * p_inv, b * p_inv], axis=0).astype(BF16))
    Z = heads(jnp.concatenate([k * p_end, b * p_end], axis=0).astype(BF16))
    Vh = heads(v.astype(BF16))
    pc = heads(jnp.exp(lpe))
    S = s_ref[0]
    A = _bdot_nt(X, Y)
    XS = _bdot_nt(X, S.astype(BF16))
    row = lax.broadcasted_iota(jnp.int32, (H, C, C), 1)
    col = lax.broadcasted_iota(jnp.int32, (H, C, C), 2)
    strict = col < row
    incl = col <= row
    a_qk = jnp.where(strict, A[:, :C, :C], 0.0)
    a_qb = jnp.where(strict, A[:, :C, C:], 0.0)
    a_rk = jnp.where(incl, A[:, C:, :C], 0.0)
    a_rb = jnp.where(incl, A[:, C:, C:], 0.0)
    rhs = XS[:, :C] + _bdot(a_qk.astype(BF16), Vh)
    U = _bdot(_unit_lower_inverse(a_qb).astype(BF16), rhs.astype(BF16))
    Ub = U.astype(BF16)
    O = XS[:, C:] + _bdot(a_rk.astype(BF16), Vh) - _bdot(a_rb.astype(BF16), Ub)
    s_ref[0] = S * pc + _bdot_tn(jnp.concatenate([Vh, -Ub], axis=1), Z)
    mu = jnp.mean(O, axis=-1, keepdims=True)
    d = O - mu
    On = d * lax.rsqrt(jnp.mean(d * d, axis=-1, keepdims=True) + RW_GN_EPS)
    On = jnp.concatenate([On[h] for h in range(H)], axis=-1)
    o_ref[0] = ((On * gng_ref[...] + gnb_ref[...]) * g_ref[0] + bg_ref[0]).astype(o_ref.dtype)


def rwkv_scan(prep, S0, gn_g, gn_b, *, B, L, C, valid_len):
    W = RW_WIDTH
    assert L % C == 0
    args = [t.reshape(B, L, W) for t in prep]
    tok = pl.BlockSpec((1, C, W), lambda b, c: (b, c, 0))
    st = pl.BlockSpec((1, RW_HEADS, RW_HEAD_DIM, RW_HEAD_DIM), lambda b, c: (b, 0, 0, 0))
    vec = pl.BlockSpec((1, W), lambda b, c: (0, 0))
    o, S = pl.pallas_call(
        functools.partial(_rw_scan_kernel, valid_len=valid_len),
        grid=(B, L // C),
        in_specs=[tok] * 8 + [st, vec, vec],
        out_specs=[tok, st],
        out_shape=[jax.ShapeDtypeStruct((B, L, W), BF16), jax.ShapeDtypeStruct(S0.shape, F32)],
        compiler_params=_cparams(("parallel", "arbitrary")),
        name="rwkv_scan",
    )(*args, S0, gn_g.reshape(1, W), gn_b.reshape(1, W))
    return o.reshape(B * L, W), S


def _hg_kernel(p_ref, s0_ref, lbp_ref, gn_ref, o_ref, s_out_ref, st_ref, *, valid_len, layer):
    C = p_ref.shape[1]
    c = pl.program_id(1)

    @pl.when(c == 0)
    def _():
        st_ref[...] = jnp.swapaxes(s0_ref[0], 1, 2)

    F = HG_F
    q = p_ref[0, :, :F]
    q = q * jax.nn.sigmoid(q)
    lbp = lbp_ref[...]
    e = jnp.exp(lbp - jnp.max(lbp, axis=0, keepdims=True))
    sm = e / jnp.sum(e, axis=0, keepdims=True)
    lb = jnp.sum(sm[1:layer + 1], axis=0, keepdims=True)
    f = lb + (1.0 - lb) * jax.nn.sigmoid(p_ref[0, :, F:2 * F])
    logf = jnp.log(f)
    k = 1.0 - f
    if valid_len < C:
        valid = lax.broadcasted_iota(jnp.int32, f.shape, 0) < valid_len
        logf = jnp.where(valid, logf, 0.0)
        k = jnp.where(valid, k, 0.0)
    trow = lax.broadcasted_iota(jnp.int32, (C, C), 0)
    tcol = lax.broadcasted_iota(jnp.int32, (C, C), 1)
    causal = tcol <= trow
    bc = jnp.dot(jnp.where(causal, 1.0, 0.0), logf, precision=HI, preferred_element_type=F32)
    bend = bc[C - 1:C, :]
    q_dec = (q * jnp.exp(bc)).astype(BF16)
    k_inv = (k * jnp.exp(-bc)).astype(BF16)
    k_end = (k * jnp.exp(bend - bc)).astype(BF16)
    dec = jnp.exp(bend)
    gn = gn_ref[...]
    for h in range(HG_HEADS):
        sk = slice(h * HG_DK, (h + 1) * HG_DK)
        sv = slice(h * HG_DV, (h + 1) * HG_DV)
        iv = p_ref[0, :, 2 * F + h * HG_DV:2 * F + (h + 1) * HG_DV].astype(BF16)
        gate = p_ref[0, :, 2 * F + D_MODEL + h * HG_DV:2 * F + D_MODEL + (h + 1) * HG_DV]
        A = jnp.where(causal, _dot_nt(q_dec[:, sk], k_inv[:, sk]), 0.0)
        St = st_ref[h]
        o = _dot(A.astype(BF16), iv) + _dot_nt(q_dec[:, sk], St.astype(BF16))
        st_ref[h] = St * dec[:, sk] + _dot_tn(iv, k_end[:, sk])
        o = _rms_rows(o, gn) * jax.nn.sigmoid(gate)
        o_ref[0, :, sv] = o.astype(o_ref.dtype)

    @pl.when(c == pl.num_programs(1) - 1)
    def _():
        s_out_ref[0] = jnp.swapaxes(st_ref[...], 1, 2)


def hgrn2(P, S0, lower_bounds, norm_g, *, B, L, C, valid_len, layer):
    assert L % C == 0
    width = P.shape[1]
    st = pl.BlockSpec((1, HG_HEADS, HG_DK, HG_DV), lambda b, c: (b, 0, 0, 0))
    o, S = pl.pallas_call(
        functools.partial(_hg_kernel, valid_len=valid_len, layer=layer),
        grid=(B, L // C),
        in_specs=[
            pl.BlockSpec((1, C, width), lambda b, c: (b, c, 0)), st,
            pl.BlockSpec(lower_bounds.shape, lambda b, c: (0, 0)),
            pl.BlockSpec((1, HG_DV), lambda b, c: (0, 0)),
        ],
        out_specs=[pl.BlockSpec((1, C, D_MODEL), lambda b, c: (b, c, 0)), st],
        out_shape=[jax.ShapeDtypeStruct((B, L, D_MODEL), BF16), jax.ShapeDtypeStruct(S0.shape, F32)],
        scratch_shapes=[pltpu.VMEM((HG_HEADS, HG_DV, HG_DK), F32)],
        compiler_params=_cparams(("parallel", "arbitrary")),
        name="hgrn2",
    )(P.reshape(B, L, width), S0, lower_bounds, norm_g.reshape(1, HG_DV))
    return o.reshape(B * L, D_MODEL), S


def _rope64(x, cos, sin_signed):
    half = MLA_ROPE // 2
    swapped = jnp.concatenate([x[:, half:], x[:, :half]], axis=-1)
    return x * cos + swapped * sin_signed


def _mla_prep_kernel(p_ref, gq_ref, wqb_ref, gkv_ref, wuk_ref, gqn_ref, gqr_ref, gkn_ref, gkr_ref, cos_ref, sin_ref,
                     q_out, k_out, c_out, cb_out, kr_out):
    p = p_ref[...]
    tm = p.shape[0]
    cos, sin = cos_ref[...], sin_ref[...]
    q = _dot(_rms_rows(p[:, :Q_LORA], gq_ref[...]).astype(BF16), wqb_ref[...])
    c = _rms_rows(p[:, Q_LORA:Q_LORA + KV_LORA], gkv_ref[...])
    kr = p[:, Q_LORA + KV_LORA:Q_LORA + KV_LORA + MLA_ROPE]
    c_out[...] = c
    cb = c.astype(BF16)
    cb_out[...] = cb
    kr_out[...] = kr
    kn = _dot(cb, wuk_ref[...])
    kr_ss = jnp.sum(kr * kr, axis=-1, keepdims=True)
    kr_rot = _rope64(kr * gkr_ref[...], cos, sin)
    zpad = jnp.zeros((tm, MLA_HEAD_PAD - MLA_QK), F32)
    nope_off = MLA_HEADS * MLA_NOPE
    for h in range(MLA_HEADS):
        qn = q[:, h * MLA_NOPE:(h + 1) * MLA_NOPE]
        qr = q[:, nope_off + h * MLA_ROPE:nope_off + (h + 1) * MLA_ROPE]
        ss = jnp.sum(qn * qn, axis=-1, keepdims=True) + jnp.sum(qr * qr, axis=-1, keepdims=True)
        inv = lax.rsqrt(ss / MLA_QK + NORM_EPS) * MLA_SCALE
        qt = jnp.concatenate([qn * inv * gqn_ref[...], _rope64(qr * inv * gqr_ref[...], cos, sin), zpad], axis=-1)
        q_out[:, h * MLA_HEAD_PAD:(h + 1) * MLA_HEAD_PAD] = qt.astype(BF16)
        knh = kn[:, h * MLA_NOPE:(h + 1) * MLA_NOPE]
        inv = lax.rsqrt((jnp.sum(knh * knh, axis=-1, keepdims=True) + kr_ss) / MLA_QK + NORM_EPS)
        kt = jnp.concatenate([knh * inv * gkn_ref[...], kr_rot * inv, zpad], axis=-1)
        k_out[:, h * MLA_HEAD_PAD:(h + 1) * MLA_HEAD_PAD] = kt.astype(BF16)


def mla_prep(P, wts, cos, sin, *, tm):
    M = P.shape[0]
    full = lambda shape: pl.BlockSpec(shape, lambda i: (0,) * len(shape))
    rowblk = lambda w: pl.BlockSpec((tm, w), lambda i: (i, 0))
    HP = MLA_HEADS * MLA_HEAD_PAD
    return pl.pallas_call(
        _mla_prep_kernel,
        grid=(M // tm,),
        in_specs=[
            pl.BlockSpec((tm, MLA_PAD), lambda i: (i, RW_PAD // MLA_PAD)),
            full((1, Q_LORA)), full((Q_LORA, MLA_HEADS * MLA_QK)), full((1, KV_LORA)),
            full((KV_LORA, MLA_HEADS * MLA_NOPE)), full((1, MLA_NOPE)), full((1, MLA_ROPE)),
            full((1, MLA_NOPE)), full((1, MLA_ROPE)), rowblk(MLA_ROPE), rowblk(MLA_ROPE),
        ],
        out_specs=[rowblk(HP), rowblk(HP), rowblk(KV_LORA), rowblk(KV_LORA), rowblk(MLA_ROPE)],
        out_shape=[jax.ShapeDtypeStruct((M, HP), BF16), jax.ShapeDtypeStruct((M, HP), BF16),
                   jax.ShapeDtypeStruct((M, KV_LORA), F32), jax.ShapeDtypeStruct((M, KV_LORA), BF16),
                   jax.ShapeDtypeStruct((M, MLA_ROPE), F32)],
        compiler_params=_cparams(("parallel",)),
        name="mla_prep",
    )(P, wts["q_norm"], wts["w_qb"], wts["kv_norm"], wts["w_uk"], wts["gq_nope"], wts["gq_rope"],
      wts["gk_nope"], wts["gk_rope"], cos, sin)


def _flash_kernel(q_ref, k_ref, c_ref, o_ref, m_ref, l_ref, acc_ref):
    i = pl.program_id(2)
    j = pl.program_id(3)
    tq, tk = q_ref.shape[0], k_ref.shape[0]

    @pl.when(j == 0)
    def _():
        m_ref[...] = jnp.full_like(m_ref, -jnp.inf)
        l_ref[...] = jnp.zeros_like(l_ref)
        acc_ref[...] = jnp.zeros_like(acc_ref)

    def update(masked):
        s = _dot_nt(q_ref[...], k_ref[...])
        if masked:
            row = lax.broadcasted_iota(jnp.int32, s.shape, 0)
            col = lax.broadcasted_iota(jnp.int32, s.shape, 1)
            s = jnp.where(col <= row, s, -jnp.inf)
        m_new = jnp.maximum(m_ref[...], jnp.max(s, axis=-1, keepdims=True))
        alpha = jnp.exp(m_ref[...] - m_new)
        p = jnp.exp(s - m_new)
        l_ref[...] = alpha * l_ref[...] + jnp.sum(p, axis=-1, keepdims=True)
        acc_ref[...] = alpha * acc_ref[...] + _dot(p.astype(BF16), c_ref[...])
        m_ref[...] = m_new

    @pl.when(j < i)
    def _():
        update(False)

    @pl.when(j == i)
    def _():
        update(True)
        o_ref[...] = (acc_ref[...] / l_ref[...]).astype(o_ref.dtype)


def flash_attention(q, k, cb, *, B, L, t):
    nb = L // t
    HP = MLA_HEADS * KV_LORA
    return pl.pallas_call(
        _flash_kernel,
        grid=(B, MLA_HEADS, nb, nb),
        in_specs=[
            pl.BlockSpec((t, MLA_HEAD_PAD), lambda b, h, i, j: (b * nb + i, h)),
            pl.BlockSpec((t, MLA_HEAD_PAD), lambda b, h, i, j: (b * nb + jnp.minimum(i, j), h)),
            pl.BlockSpec((t, KV_LORA), lambda b, h, i, j: (b * nb + jnp.minimum(i, j), 0)),
        ],
        out_specs=pl.BlockSpec((t, KV_LORA), lambda b, h, i, j: (b * nb + i, h)),
        out_shape=jax.ShapeDtypeStruct((B * L, HP), BF16),
        scratch_shapes=[pltpu.VMEM((t, 1), F32), pltpu.VMEM((t, 1), F32), pltpu.VMEM((t, KV_LORA), F32)],
        compiler_params=_cparams(("parallel", "parallel", "parallel", "arbitrary")),
        name="mla_flash",
    )(q, k, cb)


def _paged_kernel(pt_ref, *refs, n_pg):
    lat_refs = refs[:n_pg]
    kr_refs = refs[n_pg:2 * n_pg]
    (q_ref, knew_ref, cnew_ref, wukt_ref, gkn_ref, gkr_ref, cos_ref, sin_ref,
     o_ref, m_ref, l_ref, acc_ref, qabs_ref) = refs[2 * n_pg:]
    j = pl.program_id(1)
    H = MLA_HEADS
    q = q_ref[0]

    @pl.when(j == 0)
    def _():
        m_ref[...] = jnp.full_like(m_ref, -jnp.inf)
        l_ref[...] = jnp.zeros_like(l_ref)
        acc_ref[...] = jnp.zeros_like(acc_ref)
        qn = q[:, :MLA_NOPE].astype(F32) * gkn_ref[...]
        hrow = lax.broadcasted_iota(jnp.int32, (H, H * MLA_NOPE), 0)
        hcol = lax.broadcasted_iota(jnp.int32, (H, H * MLA_NOPE), 1) // MLA_NOPE
        qblk = jnp.where(hrow == hcol, jnp.concatenate([qn] * H, axis=-1), 0.0)
        qabs_ref[...] = _dot(qblk.astype(BF16), wukt_ref[...]).astype(BF16)

    c = jnp.concatenate([r[0] for r in lat_refs], axis=0)
    kr = jnp.concatenate([r[0] for r in kr_refs], axis=0)
    cb = c.astype(BF16)
    knt = _dot_nt(wukt_ref[...], cb)
    ss = jnp.concatenate(
        [jnp.sum(jnp.square(knt[h * MLA_NOPE:(h + 1) * MLA_NOPE]), axis=0, keepdims=True) for h in range(H)], axis=0)
    ss = ss + lax.dot_general(jnp.ones((H, MLA_ROPE), F32), kr * kr, (((1,), (1,)), ((), ())),
                              precision=HI, preferred_element_type=F32)
    inv = lax.rsqrt(ss / MLA_QK + NORM_EPS)
    kr_rot = _rope64(kr * gkr_ref[...], cos_ref[...], sin_ref[...]).astype(BF16)
    s = (_dot_nt(qabs_ref[...], cb) + _dot_nt(q[:, MLA_NOPE:MLA_QK], kr_rot)) * inv
    m_new = jnp.maximum(m_ref[...], jnp.max(s, axis=-1, keepdims=True))
    alpha = jnp.exp(m_ref[...] - m_new)
    p = jnp.exp(s - m_new)
    l_ref[...] = alpha * l_ref[...] + jnp.sum(p, axis=-1, keepdims=True)
    acc_ref[...] = alpha * acc_ref[...] + _dot(p.astype(BF16), cb)
    m_ref[...] = m_new

    @pl.when(j == pl.num_programs(1) - 1)
    def _():
        s_self = jnp.sum(q.astype(F32) * knew_ref[0].astype(F32), axis=-1, keepdims=True)
        m_fin = jnp.maximum(m_ref[...], s_self)
        a = jnp.exp(m_ref[...] - m_fin)
        p_self = jnp.exp(s_self - m_fin)
        l_fin = a * l_ref[...] + p_self
        acc = a * acc_ref[...] + p_self * cnew_ref[0]
        o_ref[0] = (acc / l_fin).astype(o_ref.dtype)


def paged_attention(q, k_new, c_new, cache_lat, cache_kr, page_table, w_uk_t, gk_nope, gk_rope, cos, sin, *, n_pg):
    Bd, n_pages = page_table.shape
    assert n_pages % n_pg == 0
    H = MLA_HEADS
    lat_specs = [pl.BlockSpec((1, PAGE_SIZE, KV_LORA), functools.partial(
        lambda b, j, pt, u: (pt[b, j * n_pg + u], 0, 0), u=u)) for u in range(n_pg)]
    kr_specs = [pl.BlockSpec((1, PAGE_SIZE, MLA_ROPE), functools.partial(
        lambda b, j, pt, u: (pt[b, j * n_pg + u], 0, 0), u=u)) for u in range(n_pg)]
    full = lambda shape: pl.BlockSpec(shape, lambda b, j, pt: (0,) * len(shape))
    keys = n_pg * PAGE_SIZE
    grid_spec = pltpu.PrefetchScalarGridSpec(
        num_scalar_prefetch=1,
        grid=(Bd, n_pages // n_pg),
        in_specs=lat_specs + kr_specs + [
            pl.BlockSpec((1, H, MLA_HEAD_PAD), lambda b, j, pt: (b, 0, 0)),
            pl.BlockSpec((1, H, MLA_HEAD_PAD), lambda b, j, pt: (b, 0, 0)),
            pl.BlockSpec((1, 1, KV_LORA), lambda b, j, pt: (b, 0, 0)),
            full((H * MLA_NOPE, KV_LORA)), full((1, MLA_NOPE)), full((1, MLA_ROPE)),
            pl.BlockSpec((keys, MLA_ROPE), lambda b, j, pt: (j, 0)),
            pl.BlockSpec((keys, MLA_ROPE), lambda b, j, pt: (j, 0)),
        ],
        out_specs=pl.BlockSpec((1, H, KV_LORA), lambda b, j, pt: (b, 0, 0)),
        scratch_shapes=[pltpu.VMEM((H, 1), F32), pltpu.VMEM((H, 1), F32), pltpu.VMEM((H, KV_LORA), F32),
                        pltpu.VMEM((H, KV_LORA), BF16)],
    )
    o = pl.pallas_call(
        functools.partial(_paged_kernel, n_pg=n_pg),
        grid_spec=grid_spec,
        out_shape=jax.ShapeDtypeStruct((Bd, H, KV_LORA), BF16),
        compiler_params=_cparams(("parallel", "arbitrary")),
        name="mla_paged",
    )(page_table, *([cache_lat] * n_pg), *([cache_kr] * n_pg),
      q.reshape(Bd, H, MLA_HEAD_PAD), k_new.reshape(Bd, H, MLA_HEAD_PAD), c_new.reshape(Bd, 1, KV_LORA),
      w_uk_t, gk_nope, gk_rope, cos, sin)
    return o.reshape(Bd, H * KV_LORA)


def _uv_kernel(o_ref, w_ref, out_ref):
    out_ref[...] = _dot(o_ref[...], w_ref[0]).astype(out_ref.dtype)


def mla_uv(o_lat, w_uv, *, tm):
    M = o_lat.shape[0]
    return pl.pallas_call(
        _uv_kernel,
        grid=(M // tm, MLA_HEADS),
        in_specs=[pl.BlockSpec((tm, KV_LORA), lambda i, h: (i, h)),
                  pl.BlockSpec((1, KV_LORA, MLA_V), lambda i, h: (h, 0, 0))],
        out_specs=pl.BlockSpec((tm, MLA_V), lambda i, h: (i, h)),
        out_shape=jax.ShapeDtypeStruct((M, MLA_WIDTH), BF16),
        compiler_params=_cparams(("parallel", "arbitrary")),
        name="mla_uv",
    )(o_lat, w_uv)


def _rope_tables(pos):
    half = MLA_ROPE // 2
    inv = ROPE_THETA ** (-jnp.arange(half, dtype=F32) / half)
    ang = pos.astype(F32)[:, None] * inv
    cos, sin = jnp.cos(ang), jnp.sin(ang)
    return jnp.concatenate([cos, cos], axis=-1), jnp.concatenate([-sin, sin], axis=-1)


def _pad_cols(w, n):
    return jnp.pad(w, ((0, 0), (0, n - w.shape[1])))


def _pad_rows(w, n):
    return jnp.pad(w, ((0, n - w.shape[0]), (0, 0)))


def _rw_relayout(t):
    o1 = 3 * RW_WIDTH
    o2 = o1 + RW_DECAY_LORA
    o3 = o2 + RW_A_LORA
    pad = [(0, 0)] * (t.ndim - 1) + [(0, RW_LORA_PAD - RW_DECAY_LORA)]
    return jnp.concatenate([t[..., :o1], jnp.pad(t[..., o1:o2], pad), jnp.pad(t[..., o2:o3], pad), t[..., o3:]], axis=-1)


def _rw_unlayout(t):
    o1 = 3 * RW_WIDTH
    o2 = o1 + RW_LORA_PAD
    o3 = o2 + RW_LORA_PAD
    return jnp.concatenate([t[..., :o1], t[..., o1:o1 + RW_DECAY_LORA], t[..., o2:o2 + RW_A_LORA], t[..., o3:]], axis=-1)


def _prepare_weights(W):
    row = lambda t: t.reshape(1, -1).astype(F32)
    w_in_even = W["w_in_even"]
    w_even = jnp.concatenate([_rw_relayout(w_in_even[:, :RW_PROJ]), _pad_cols(w_in_even[:, RW_PROJ:], MLA_PAD)], axis=1)
    head_of = jnp.arange(RW_WIDTH) // RW_HEAD_DIM
    seg = (head_of[:, None] == jnp.arange(LANE)[None, :]).astype(F32)
    rw = dict(
        mu=row(_rw_relayout(W["rw_mu"])), w0=row(W["rw_w0"]),
        w_dec=_pad_rows(W["rw_w_dec"], RW_LORA_PAD).astype(BF16), a0=row(W["rw_a0"]),
        w_a=_pad_rows(W["rw_w_a"], RW_LORA_PAD).astype(BF16), w_g=W["rw_w_g"].astype(BF16),
        k_k=row(W["rw_k_k"]), k_a=row(W["rw_k_a"]), r_k=row(W["rw_r_k"]), seg=seg, segt=seg.T,
    )
    wqb = W["mla_w_qb"].reshape(Q_LORA, MLA_HEADS, MLA_QK)
    wqb = jnp.concatenate([wqb[:, :, :MLA_NOPE].reshape(Q_LORA, -1), wqb[:, :, MLA_NOPE:].reshape(Q_LORA, -1)], axis=1)
    mla = dict(
        q_norm=row(W["mla_q_norm"]), w_qb=wqb.astype(BF16), kv_norm=row(W["mla_kv_norm"]),
        w_uk=W["mla_w_uk"].astype(BF16), w_uk_t=W["mla_w_uk"].T.astype(BF16),
        gq_nope=row(W["mla_qk_norm_q"][:MLA_NOPE]), gq_rope=row(W["mla_qk_norm_q"][MLA_NOPE:]),
        gk_nope=row(W["mla_qk_norm_k"][:MLA_NOPE]), gk_rope=row(W["mla_qk_norm_k"][MLA_NOPE:]),
        w_uv=W["mla_w_uv"].astype(BF16),
    )
    return dict(
        w_even=w_even.astype(BF16), rw=rw, mla=mla,
        w_out_rw=W["w_out_even"][:RW_WIDTH].astype(BF16), w_out_mla=W["w_out_even"][RW_WIDTH:].astype(BF16),
        w_in_odd=W["w_in_odd"].astype(BF16), w_out_odd=W["w_out_odd"].astype(BF16),
        ffn_w_in=W["ffn_w_in"].astype(BF16), ffn_w_out=W["ffn_w_out"].astype(BF16),
    )


def _pad_seq(t, B, L, Lp):
    if Lp == L:
        return t
    return jnp.pad(t.reshape(B, L, -1), ((0, 0), (0, Lp - L), (0, 0))).reshape(B * Lp, -1)


def _unpad_seq(t, B, L, Lp):
    if Lp == L:
        return t
    return t.reshape(B, Lp, -1)[:, :L].reshape(B * L, -1)


def _trunk(x, pos, rw_S0, rw_shift0, hg_S0, conv0, past, W, P, *, tm, chunk, t_attn):
    B, L, D = x.shape
    M = B * L
    x = x.reshape(M, D)
    Lp = max(L, 8)
    C = min(chunk, Lp)
    conv_rows = []

    Pe = norm_matmul(x, W["norm_mix_g"][0], P["w_even"], tm=tm, tn=MLA_PAD)
    prep = rwkv_prep(Pe, _rw_relayout(rw_shift0), P["rw"], seq_len=L, tm=min(tm, 256))
    prep = [_pad_seq(t, B, L, Lp) for t in prep]
    o_rw, rw_S = rwkv_scan(prep, rw_S0, W["rw_gn_g"], W["rw_gn_b"], B=B, L=Lp, C=C, valid_len=min(L, C))
    o_rw = _unpad_seq(o_rw, B, L, Lp)
    rw_shift = _rw_unlayout(Pe.reshape(B, L, EVEN_PAD)[:, -1, :RW_PAD])

    cos, sin = _rope_tables(pos)
    cos, sin = jnp.tile(cos, (B, 1)), jnp.tile(sin, (B, 1))
    q, k, c, cb, kr = mla_prep(Pe, P["mla"], cos, sin, tm=tm)
    if past is None:
        o_lat = flash_attention(q, k, cb, B=B, L=L, t=t_attn)
    else:
        cache_lat, cache_kr, page_table = past
        n_past = page_table.shape[1] * PAGE_SIZE
        cos_p, sin_p = _rope_tables(jnp.arange(n_past, dtype=jnp.int32))
        o_lat = paged_attention(q, k, c, cache_lat, cache_kr, page_table, P["mla"]["w_uk_t"],
                                P["mla"]["gk_nope"], P["mla"]["gk_rope"], cos_p, sin_p, n_pg=8)
    o_mla = mla_uv(o_lat, P["mla"]["w_uv"], tm=tm)
    x = matmul_res([(o_rw, P["w_out_rw"]), (o_mla, P["w_out_mla"])], x, tm=tm, tn=1024)
    x, tail = conv_ffn(x, conv0[0], W["norm_ffn_g"][0], P["ffn_w_in"][0], W["ffn_conv_w"][0], W["ffn_conv_b"][0],
                       P["ffn_w_out"][0], seq_len=L, tm=tm, tf=512)
    conv_rows.append(tail)

    Po = norm_matmul(x, W["norm_mix_g"][1], P["w_in_odd"], tm=tm, tn=1024)
    o_hg, hg_S = hgrn2(_pad_seq(Po, B, L, Lp), hg_S0, W["hg_lower_bounds"], W["hg_norm_g"],
                       B=B, L=Lp, C=C, valid_len=min(L, C), layer=1)
    o_hg = _unpad_seq(o_hg, B, L, Lp)
    x = matmul_res([(o_hg, P["w_out_odd"])], x, tm=tm, tn=1024)
    x, tail = conv_ffn(x, conv0[1], W["norm_ffn_g"][1], P["ffn_w_in"][1], W["ffn_conv_w"][1], W["ffn_conv_b"][1],
                       P["ffn_w_out"][1], seq_len=L, tm=tm, tf=512)
    conv_rows.append(tail)

    if L == 1:
        conv_new = jnp.stack([jnp.stack([conv0[l][:, 1, :], conv_rows[l]], axis=1) for l in range(2)])
    else:
        conv_new = jnp.stack(conv_rows)
    return (x.reshape(B, L, D), c.reshape(B, L, KV_LORA), kr.reshape(B, L, MLA_ROPE), rw_S, rw_shift, hg_S, conv_new)


def kernel(x_prompt, x_sample, cache_mla_latent, cache_mla_krope, state_rwkv, state_rwkv_shift, state_hgrn, state_ffn_conv, page_table, norm_mix_g, norm_ffn_g, w_in_even, rw_mu, rw_w0, rw_w_dec, rw_a0, rw_w_a, rw_w_g, rw_k_k, rw_k_a, rw_r_k, rw_gn_g, rw_gn_b, mla_q_norm, mla_w_qb, mla_kv_norm, mla_w_uk, mla_w_uv, mla_qk_norm_q, mla_qk_norm_k, w_out_even, w_in_odd, hg_lower_bounds, hg_norm_g, w_out_odd, ffn_w_in, ffn_conv_w, ffn_conv_b, ffn_w_out):
    W = dict(norm_mix_g=norm_mix_g, norm_ffn_g=norm_ffn_g, w_in_even=w_in_even, rw_mu=rw_mu,
             rw_w0=rw_w0, rw_w_dec=rw_w_dec, rw_a0=rw_a0, rw_w_a=rw_w_a, rw_w_g=rw_w_g,
             rw_k_k=rw_k_k, rw_k_a=rw_k_a, rw_r_k=rw_r_k, rw_gn_g=rw_gn_g, rw_gn_b=rw_gn_b,
             mla_q_norm=mla_q_norm, mla_w_qb=mla_w_qb, mla_kv_norm=mla_kv_norm, mla_w_uk=mla_w_uk,
             mla_w_uv=mla_w_uv, mla_qk_norm_q=mla_qk_norm_q, mla_qk_norm_k=mla_qk_norm_k,
             w_out_even=w_out_even, w_in_odd=w_in_odd, hg_lower_bounds=hg_lower_bounds,
             hg_norm_g=hg_norm_g, w_out_odd=w_out_odd, ffn_w_in=ffn_w_in, ffn_conv_w=ffn_conv_w,
             ffn_conv_b=ffn_conv_b, ffn_w_out=ffn_w_out)
    P = _prepare_weights(W)
    B, L, _ = x_prompt.shape
    Bd, Ld, _ = x_sample.shape
    dt = x_prompt.dtype
    y_p, lat_p, kr_p, rw_p, sh_p, hg_p, cv_p = _trunk(
        x_prompt, jnp.arange(L, dtype=jnp.int32),
        jnp.zeros((B, RW_HEADS, RW_HEAD_DIM, RW_HEAD_DIM), dt), jnp.zeros((B, RW_PROJ), dt),
        jnp.zeros((B, HG_HEADS, HG_DK, HG_DV), dt), jnp.zeros((2, B, CONV_W - 1, D_FF), dt),
        None, W, P, tm=512, chunk=64, t_attn=512)
    n_past = page_table.shape[1] * PAGE_SIZE
    y_s, lat_s, kr_s, rw_s, sh_s, hg_s, cv_s = _trunk(
        x_sample, n_past + jnp.arange(Ld, dtype=jnp.int32),
        state_rwkv, state_rwkv_shift, state_hgrn, state_ffn_conv,
        (cache_mla_latent, cache_mla_krope, page_table), W, P, tm=128, chunk=64, t_attn=512)
    return (y_p, y_s, lat_p, lat_s, kr_p, kr_s, rw_p, rw_s, sh_p, sh_s, hg_p, hg_s, cv_p, cv_s)
```
